```python
import jax, jax.numpy as jnp
from jax import lax
import numpy as np

D_MODEL = 1024
BATCH = 8
SEQ = 2048
DEPTH = 1
DEC_BATCH = 128
DEC_SEQ = 1
PAST_LEN = 16384
PAGE_SIZE = 128

D_A = D_MODEL // 2
G_A = 4
GC_A = D_A // G_A
WINDOWS = (2, 4, 8, 16)
POOL_BUF = max(WINDOWS) - 1
D_B = D_MODEL // 2
G_B = 4
C_B = D_B // G_B
CHUNK = 128
D_IN = 2 * D_A + 3 * D_B + 2 * D_MODEL
EPS = 1e-6

kernel_name = "pool_sgu_hybrid_decoder_step"


def rmsnorm(x, g):
    xf = x.astype(jnp.float32)
    r = lax.rsqrt(jnp.mean(xf * xf, axis=-1, keepdims=True) + EPS)
    return (xf * r).astype(x.dtype) * g


def layernorm(x, g, b):
    xf = x.astype(jnp.float32)
    mu = jnp.mean(xf, axis=-1, keepdims=True)
    var = jnp.mean(jnp.square(xf - mu), axis=-1, keepdims=True)
    return ((xf - mu) * lax.rsqrt(var + EPS)).astype(x.dtype) * g + b


def pool_mixer(a, prefix, pos0, pool_w, pool_b, pool_scale):
    B, T, _ = a.shape
    P = POOL_BUF
    ext = jnp.concatenate([prefix, a], axis=1)
    cs = jnp.cumsum(ext.astype(jnp.float32), axis=1)
    cs0 = jnp.concatenate([jnp.zeros((B, 1, D_A), jnp.float32), cs], axis=1)
    pos = pos0 + jnp.arange(T)
    outs = []
    for k, w in enumerate(WINDOWS):
        sl = slice(k * GC_A, (k + 1) * GC_A)
        s = cs0[:, P + 1:P + 1 + T, sl] - cs0[:, P + 1 - w:P + 1 - w + T, sl]
        cnt = jnp.minimum(pos + 1, w).astype(jnp.float32)
        outs.append(s / cnt[None, :, None])
    pooled = jnp.concatenate(outs, axis=-1).astype(a.dtype) - a
    y = jnp.einsum('btgc,gcd->btgd', pooled.reshape(B, T, G_A, GC_A), pool_w) + pool_b
    return y.reshape(B, T, D_A) * pool_scale, ext[:, -P:]


def spatial_gate(u, v, ln_g, ln_b, sgu_w, sgu_b):
    u = jax.nn.gelu(u, approximate=False)
    v = layernorm(jax.nn.gelu(v, approximate=False), ln_g, ln_b)
    B, T, _ = v.shape
    Lc = min(T, CHUNK)
    n = T // Lc
    vr = v.reshape(B, n, Lc, G_B, C_B)
    tril = jnp.tril(jnp.ones((Lc, Lc), dtype=bool))
    W = jnp.where(tril[None], sgu_w[:, :Lc, :Lc], 0.0).astype(v.dtype)
    s = jnp.einsum('gts,bnsgc->bntgc', W, vr) + sgu_b[:, :Lc].T[:, :, None]
    return u * s.reshape(B, T, D_B), v[:, T - Lc:]


def layer(x, c, prefix, pos0, w_ada, b_ada, norm_gain, w_in, pool_w, pool_b, pool_scale,
          sgu_ln_g, sgu_ln_b, sgu_w, sgu_b, w_branch_a, w_branch_b, w_out):
    mod = jax.nn.silu(c) @ w_ada + b_ada
    shift, scale, gate = jnp.split(mod, 3, axis=-1)
    h = rmsnorm(x, norm_gain) * (1.0 + scale[:, None]) + shift[:, None]
    z = h @ w_in
    cuts = np.cumsum([D_A, D_A, D_B, D_B, D_B, D_MODEL])
    a, ga, u, v, gb, ma, mb = jnp.split(z, [int(i) for i in cuts], axis=-1)
    ya, new_pool = pool_mixer(a, prefix, pos0, pool_w, pool_b, pool_scale)
    ya = ya * jax.nn.silu(ga)
    yb, new_v = spatial_gate(u, v, sgu_ln_g, sgu_ln_b, sgu_w, sgu_b)
    yb = yb * jax.nn.silu(gb)
    merged = jax.nn.sigmoid(ma) * (ya @ w_branch_a) + jax.nn.sigmoid(mb) * (yb @ w_branch_b)
    x = x + gate[:, None] * (merged @ w_out)
    return x, new_pool, new_v


def setup_inputs(seed: int = 0) -> dict:
    key = jax.random.key(seed)
    ks = jax.random.split(key, 24)
    f = jnp.float32
    nrm = lambda k, s, sc: jax.random.normal(k, s, f) * sc
    return {
        "x_prompt": nrm(ks[0], (BATCH, SEQ, D_MODEL), 1.0),
        "x_sample": nrm(ks[1], (DEC_BATCH, DEC_SEQ, D_MODEL), 1.0),
        "state_pool": nrm(ks[2], (DEPTH, DEC_BATCH, POOL_BUF, D_A), 0.6),
        "c_prompt": nrm(ks[3], (BATCH, D_MODEL), 1.0),
        "c_sample": nrm(ks[4], (DEC_BATCH, D_MODEL), 1.0),
        "w_ada": nrm(ks[5], (DEPTH, D_MODEL, 3 * D_MODEL), 0.5 * D_MODEL ** -0.5),
        "b_ada": nrm(ks[6], (DEPTH, 3 * D_MODEL), 0.02),
        "norm_gain": 1.0 + nrm(ks[7], (DEPTH, D_MODEL), 0.05),
        "w_in": nrm(ks[8], (DEPTH, D_MODEL, D_IN), D_MODEL ** -0.5),
        "pool_w": nrm(ks[9], (DEPTH, G_A, GC_A, GC_A), GC_A ** -0.5),
        "pool_b": nrm(ks[10], (DEPTH, G_A, GC_A), 0.02),
        "pool_scale": 1.0 + nrm(ks[11], (DEPTH, D_A), 0.1),
        "sgu_ln_g": 1.0 + nrm(ks[12], (DEPTH, D_B), 0.05),
        "sgu_ln_b": nrm(ks[13], (DEPTH, D_B), 0.02),
        "sgu_w": nrm(ks[14], (DEPTH, G_B, CHUNK, CHUNK), CHUNK ** -0.5),
        "sgu_b": 1.0 + nrm(ks[15], (DEPTH, G_B, CHUNK), 0.1),
        "w_branch_a": nrm(ks[16], (DEPTH, D_A, D_MODEL), D_A ** -0.5),
        "w_branch_b": nrm(ks[17], (DEPTH, D_B, D_MODEL), D_B ** -0.5),
        "w_out": nrm(ks[18], (DEPTH, D_MODEL, D_MODEL), D_MODEL ** -0.5),
        "final_gain": 1.0 + nrm(ks[19], (D_MODEL,), 0.05),
    }


def reference(x_prompt, x_sample, state_pool, c_prompt, c_sample, w_ada, b_ada, norm_gain, w_in,
              pool_w, pool_b, pool_scale, sgu_ln_g, sgu_ln_b, sgu_w, sgu_b, w_branch_a, w_branch_b,
              w_out, final_gain):
    xp, xs = x_prompt, x_sample
    pool_p, pool_s, v_p, v_s = [], [], [], []
    zero_prefix = jnp.zeros((BATCH, POOL_BUF, D_A), x_prompt.dtype)
    for l in range(DEPTH):
        params = (w_ada[l], b_ada[l], norm_gain[l], w_in[l], pool_w[l], pool_b[l], pool_scale[l],
                  sgu_ln_g[l], sgu_ln_b[l], sgu_w[l], sgu_b[l], w_branch_a[l], w_branch_b[l], w_out[l])
        xp, npp, nvp = layer(xp, c_prompt, zero_prefix, 0, *params)
        xs, nps, nvs = layer(xs, c_sample, state_pool[l], PAST_LEN, *params)
        pool_p.append(npp); pool_s.append(nps); v_p.append(nvp); v_s.append(nvs)
    y_prompt = rmsnorm(xp, final_gain)
    y_sample = rmsnorm(xs, final_gain)
    return (y_prompt, y_sample, jnp.stack(pool_p), jnp.stack(pool_s), jnp.stack(v_p), jnp.stack(v_s))
```

```python
import functools

import jax
import jax.numpy as jnp
import numpy as np
from jax import lax
from jax.experimental import pallas as pl
from jax.experimental.pallas import tpu as pltpu

G_A = 4
WINDOWS = (2, 4, 8, 16)
POOL_BUF = max(WINDOWS) - 1
CARRY_ROWS = POOL_BUF + 1
G_B = 4
CHUNK = 128
EPS = 1e-6
SQRT_HALF = np.float32(np.sqrt(0.5))

V7X_LANES = 128
V7X_VMEM_BYTES = 64 * 1024 * 1024

BF16 = jnp.bfloat16
F32 = jnp.float32


def _dot(a, b):
    return jnp.dot(a, b, preferred_element_type=F32)


def _gelu(x):
    return 0.5 * x * (1.0 + lax.erf(x * SQRT_HALF))


def _silu(x):
    return x * jax.nn.sigmoid(x)


def _rms_scale(x):
    return lax.rsqrt(jnp.mean(x * x, axis=-1, keepdims=True) + EPS)


def _layernorm(x, g, b):
    mu = jnp.mean(x, axis=-1, keepdims=True)
    xc = x - mu
    var = jnp.mean(xc * xc, axis=-1, keepdims=True)
    return xc * lax.rsqrt(var + EPS) * g + b


def _lane_groups(x, n):
    w = x.shape[-1] // n
    return [x[:, i * w:(i + 1) * w] for i in range(n)]


def _ada_kernel(c_ref, w_ref, b_ref, o_ref):
    s = _silu(c_ref[...]).astype(BF16)
    o_ref[...] = _dot(s, w_ref[...].astype(BF16)) + b_ref[...]


def _ada_call(c_all, w_ada, b_ada, *, block_n):
    rows, d = c_all.shape
    n = w_ada.shape[1]
    return pl.pallas_call(
        _ada_kernel,
        grid=(n // block_n,),
        in_specs=[
            pl.BlockSpec((rows, d), lambda j: (0, 0)),
            pl.BlockSpec((d, block_n), lambda j: (0, j)),
            pl.BlockSpec((1, block_n), lambda j: (0, j)),
        ],
        out_specs=pl.BlockSpec((rows, block_n), lambda j: (0, j)),
        out_shape=jax.ShapeDtypeStruct((rows, n), F32),
        compiler_params=pltpu.CompilerParams(dimension_semantics=("arbitrary",)),
        name="ada_mod",
    )(c_all, w_ada, b_ada)


def _modulated_input(x, mod, gain):
    d = x.shape[-1]
    shift, scale, gate = mod[:, :d], mod[:, d:2 * d], mod[:, 2 * d:]
    h = x * _rms_scale(x) * gain * (1.0 + scale) + shift
    return h, gate


def _pool_project(pooled_minus_a, ga, pool_w_ref, pool_b, pool_scale):
    parts = _lane_groups(pooled_minus_a.astype(BF16), G_A)
    y = jnp.concatenate([_dot(parts[g], pool_w_ref[g]) for g in range(G_A)], axis=-1)
    return (y + pool_b) * pool_scale * _silu(ga)


def _merge_and_output(x, gate, ya, yb, ma, mb, wba_ref, wbb_ref, wout_ref, fgain):
    pa = _dot(ya.astype(BF16), wba_ref[...])
    pb = _dot(yb.astype(BF16), wbb_ref[...])
    merged = jax.nn.sigmoid(ma) * pa + jax.nn.sigmoid(mb) * pb
    xn = x + gate * _dot(merged.astype(BF16), wout_ref[...])
    return xn * _rms_scale(xn) * fgain


def _prompt_kernel(x_ref, mod_ref, gain_ref, w_in_ref, pool_w_ref, pool_b_ref, pool_scale_ref,
                   ln_g_ref, ln_b_ref, sgu_w_ref, sgu_bias_ref, wba_ref, wbb_ref, wout_ref, fgain_ref,
                   y_ref, pool_out_ref, v_out_ref,
                   band_ref, carry_ref, sgu_wm_ref, prev_ref, *, tm):
    b = pl.program_id(0)
    t = pl.program_id(1)
    n_t = pl.num_programs(1)
    d_a = pool_b_ref.shape[-1]
    d_b = ln_g_ref.shape[-1]
    gc = d_a // G_A

    @pl.when((b == 0) & (t == 0))
    def _build_constant_matrices():
        r = lax.broadcasted_iota(jnp.int32, (tm, tm), 0)
        j = lax.broadcasted_iota(jnp.int32, (tm, tm), 1)
        rc = lax.broadcasted_iota(jnp.int32, (CARRY_ROWS, V7X_LANES), 0)
        jc = lax.broadcasted_iota(jnp.int32, (CARRY_ROWS, V7X_LANES), 1)
        for g, w in enumerate(WINDOWS):
            band_ref[g] = jnp.where((r >= j) & (r - j < w), 1.0 / w, 0.0).astype(BF16)
            carry_ref[g] = jnp.where((jc < CARRY_ROWS) & (jc - rc > CARRY_ROWS - w), 1.0 / w, 0.0).astype(BF16)
        rs = lax.broadcasted_iota(jnp.int32, (CHUNK, CHUNK), 0)
        js = lax.broadcasted_iota(jnp.int32, (CHUNK, CHUNK), 1)
        for g in range(G_B):
            sgu_wm_ref[g] = jnp.where(rs >= js, sgu_w_ref[g], 0.0).astype(BF16)

    x = x_ref[0]
    h, gate = _modulated_input(x, mod_ref[0], gain_ref[...])
    hb = h.astype(BF16)

    def z_cols(lo, width):
        return _dot(hb, w_in_ref[:, lo:lo + width])

    a = z_cols(0, d_a)
    ga = z_cols(d_a, d_a)
    first_tile = t == 0
    prev = jnp.where(first_tile, 0.0, prev_ref[...])
    a_parts = _lane_groups(a.astype(BF16), G_A)
    prev_parts = _lane_groups(prev.astype(BF16), G_A)
    zero_rows = jnp.zeros((V7X_LANES - CARRY_ROWS, gc), BF16)
    pooled_parts = []
    for g in range(G_A):
        main = _dot(band_ref[g], a_parts[g])
        carried = _dot(carry_ref[g], jnp.concatenate([prev_parts[g], zero_rows], axis=0))
        pooled_parts.append(jnp.concatenate([main[:CARRY_ROWS] + carried, main[CARRY_ROWS:]], axis=0))
    pooled = jnp.concatenate(pooled_parts, axis=-1)
    row = lax.broadcasted_iota(jnp.int32, (CARRY_ROWS, d_a), 0)
    lane = lax.broadcasted_iota(jnp.int32, (CARRY_ROWS, d_a), 1)
    win = jnp.full((CARRY_ROWS, d_a), WINDOWS[0], jnp.int32)
    for g, w in enumerate(WINDOWS[1:], start=1):
        win = jnp.where(lane >= g * gc, w, win)
    fix = jnp.where(first_tile, win.astype(F32) / jnp.minimum(row + 1, win).astype(F32), 1.0)
    pooled = jnp.concatenate([pooled[:CARRY_ROWS] * fix, pooled[CARRY_ROWS:]], axis=0)
    ya = _pool_project(pooled - a, ga, pool_w_ref, pool_b_ref[...], pool_scale_ref[...])
    prev_ref[...] = a[tm - CARRY_ROWS:]

    u = _gelu(z_cols(2 * d_a, d_b))
    vn = _layernorm(_gelu(z_cols(2 * d_a + d_b, d_b)), ln_g_ref[...], ln_b_ref[...])
    gb = z_cols(2 * d_a + 2 * d_b, d_b)
    vb = vn.astype(BF16)
    s_chunks = []
    for c in range(tm // CHUNK):
        v_parts = _lane_groups(vb[c * CHUNK:(c + 1) * CHUNK], G_B)
        s_c = jnp.concatenate([_dot(sgu_wm_ref[g], v_parts[g]) for g in range(G_B)], axis=-1)
        s_chunks.append(s_c + sgu_bias_ref[...])
    yb = u * jnp.concatenate(s_chunks, axis=0) * _silu(gb)

    d = x.shape[-1]
    m_lo = 2 * d_a + 3 * d_b
    ma = z_cols(m_lo, d)
    mb = z_cols(m_lo + d, d)
    y_ref[0] = _merge_and_output(x, gate, ya, yb, ma, mb, wba_ref, wbb_ref, wout_ref, fgain_ref[...])

    @pl.when(t == n_t - 1)
    def _emit_sequence_state():
        pool_out_ref[0] = a[tm - CARRY_ROWS:]
        v_out_ref[0] = vn[tm - CHUNK:]


def _prompt_call(x, mod, gain, w_in, pool_w, pool_b, pool_scale, ln_g, ln_b, sgu_w, sgu_bias,
                 wba, wbb, wout, fgain, *, tm):
    bsz, seq, d = x.shape
    d_in = w_in.shape[1]
    d_a = pool_b.shape[-1]
    d_b = ln_g.shape[-1]
    gc = d_a // G_A
    assert seq % tm == 0 and tm % CHUNK == 0 and seq % CHUNK == 0
    assert gc == V7X_LANES and d_b // G_B == V7X_LANES

    def const(shape):
        nd = len(shape)
        return pl.BlockSpec(shape, lambda b, t: (0,) * nd, pipeline_mode=pl.Buffered(1))

    in_specs = [
        pl.BlockSpec((1, tm, d), lambda b, t: (b, t, 0)),
        pl.BlockSpec((1, 1, 3 * d), lambda b, t: (b, 0, 0)),
        const((1, d)),
        const((d, d_in)),
        const((G_A, gc, gc)),
        const((1, d_a)),
        const((1, d_a)),
        const((1, d_b)),
        const((1, d_b)),
        const((G_B, CHUNK, CHUNK)),
        const((CHUNK, d_b)),
        const((d_a, d)),
        const((d_b, d)),
        const((d, d)),
        const((1, d)),
    ]
    out_specs = [
        pl.BlockSpec((1, tm, d), lambda b, t: (b, t, 0)),
        pl.BlockSpec((1, CARRY_ROWS, d_a), lambda b, t: (b, 0, 0)),
        pl.BlockSpec((1, CHUNK, d_b), lambda b, t: (b, 0, 0)),
    ]
    out_shape = [
        jax.ShapeDtypeStruct((bsz, seq, d), F32),
        jax.ShapeDtypeStruct((bsz, CARRY_ROWS, d_a), F32),
        jax.ShapeDtypeStruct((bsz, CHUNK, d_b), F32),
    ]
    scratch = [
        pltpu.VMEM((G_A, tm, tm), BF16),
        pltpu.VMEM((G_A, CARRY_ROWS, V7X_LANES), BF16),
        pltpu.VMEM((G_B, CHUNK, CHUNK), BF16),
        pltpu.VMEM((CARRY_ROWS, d_a), F32),
    ]
    return pl.pallas_call(
        functools.partial(_prompt_kernel, tm=tm),
        grid=(bsz, seq // tm),
        in_specs=in_specs,
        out_specs=out_specs,
        out_shape=out_shape,
        scratch_shapes=scratch,
        compiler_params=pltpu.CompilerParams(
            dimension_semantics=("arbitrary", "arbitrary"),
            vmem_limit_bytes=V7X_VMEM_BYTES * 3 // 4,
        ),
        name="prompt_layer",
    )(x, mod, gain, w_in, pool_w, pool_b, pool_scale, ln_g, ln_b, sgu_w, sgu_bias, wba, wbb, wout, fgain)


def _sample_kernel(x_ref, mod_ref, state_ref, gain_ref, w_in_ref, pool_w_ref, pool_b_ref, pool_scale_ref,
                   ln_g_ref, ln_b_ref, sgu_w0_ref, sgu_b0_ref, wba_ref, wbb_ref, wout_ref, fgain_ref,
                   y_ref, state_out_ref, v_out_ref):
    d_a = pool_b_ref.shape[-1]
    d_b = ln_g_ref.shape[-1]
    gc = d_a // G_A
    x = x_ref[...]
    d = x.shape[-1]
    h, gate = _modulated_input(x, mod_ref[...], gain_ref[...])
    hb = h.astype(BF16)

    def z_cols(lo, width):
        return _dot(hb, w_in_ref[:, lo:lo + width])

    a = z_cols(0, d_a)
    ga = z_cols(d_a, d_a)
    a_parts = _lane_groups(a, G_A)
    pooled_parts = []
    for g, w in enumerate(WINDOWS):
        acc = a_parts[g]
        for j in range(POOL_BUF - (w - 1), POOL_BUF):
            acc = acc + state_ref[:, j * d_a + g * gc:j * d_a + (g + 1) * gc]
        pooled_parts.append(acc * (1.0 / w))
    pooled = jnp.concatenate(pooled_parts, axis=-1)
    ya = _pool_project(pooled - a, ga, pool_w_ref, pool_b_ref[...], pool_scale_ref[...])
    state_out_ref[:, :(POOL_BUF - 1) * d_a] = state_ref[:, d_a:]
    state_out_ref[:, (POOL_BUF - 1) * d_a:] = a

    u = _gelu(z_cols(2 * d_a, d_b))
    vn = _layernorm(_gelu(z_cols(2 * d_a + d_b, d_b)), ln_g_ref[...], ln_b_ref[...])
    gb = z_cols(2 * d_a + 2 * d_b, d_b)
    v_out_ref[...] = vn
    yb = u * (vn * sgu_w0_ref[...] + sgu_b0_ref[...]) * _silu(gb)

    m_lo = 2 * d_a + 3 * d_b
    ma = z_cols(m_lo, d)
    mb = z_cols(m_lo + d, d)
    y_ref[...] = _merge_and_output(x, gate, ya, yb, ma, mb, wba_ref, wbb_ref, wout_ref, fgain_ref[...])


def _sample_call(x, mod, state, gain, w_in, pool_w, pool_b, pool_scale, ln_g, ln_b, sgu_w0, sgu_b0,
                 wba, wbb, wout, fgain):
    rows, d = x.shape
    d_b = ln_g.shape[-1]
    args = (x, mod, state, gain, w_in, pool_w, pool_b, pool_scale, ln_g, ln_b, sgu_w0, sgu_b0, wba, wbb, wout, fgain)

    def full(arr):
        nd = arr.ndim
        return pl.BlockSpec(arr.shape, lambda i: (0,) * nd)

    out_shape = [
        jax.ShapeDtypeStruct((rows, d), F32),
        jax.ShapeDtypeStruct(state.shape, F32),
        jax.ShapeDtypeStruct((rows, d_b), F32),
    ]
    return pl.pallas_call(
        _sample_kernel,
        grid=(1,),
        in_specs=[full(a) for a in args],
        out_specs=[pl.BlockSpec(s.shape, lambda i: (0, 0)) for s in out_shape],
        out_shape=out_shape,
        compiler_params=pltpu.CompilerParams(
            dimension_semantics=("arbitrary",),
            vmem_limit_bytes=V7X_VMEM_BYTES * 3 // 4,
        ),
        name="sample_layer",
    )(*args)


def kernel(x_prompt, x_sample, state_pool, c_prompt, c_sample, w_ada, b_ada, norm_gain, w_in, pool_w, pool_b,
           pool_scale, sgu_ln_g, sgu_ln_b, sgu_w, sgu_b, w_branch_a, w_branch_b, w_out, final_gain):
    depth = w_ada.shape[0]
    assert depth == 1, "single-layer trunk only"
    bsz, seq, d = x_prompt.shape
    dec, dec_seq, _ = x_sample.shape
    assert dec_seq == 1
    d_a = pool_b.shape[1] * pool_b.shape[2]
    d_b = sgu_ln_g.shape[1]
    gc_b = d_b // G_B

    mod = _ada_call(jnp.concatenate([c_prompt, c_sample], axis=0), w_ada[0], b_ada, block_n=d * 3 // 4)

    w_in_b = w_in[0].astype(BF16)
    pool_w_b = pool_w[0].astype(BF16)
    wba = w_branch_a[0].astype(BF16)
    wbb = w_branch_b[0].astype(BF16)
    wout = w_out[0].astype(BF16)
    pool_b2 = pool_b.reshape(1, d_a)
    fgain = final_gain.reshape(1, d)
    sgu_bias = jnp.repeat(sgu_b[0].T, gc_b, axis=1)

    y_p, pool_p, v_p = _prompt_call(
        x_prompt, mod[:bsz].reshape(bsz, 1, 3 * d), norm_gain, w_in_b, pool_w_b, pool_b2, pool_scale,
        sgu_ln_g, sgu_ln_b, sgu_w[0], sgu_bias, wba, wbb, wout, fgain, tm=256)

    sgu_w0 = jnp.repeat(sgu_w[0, :, 0, 0], gc_b).reshape(1, d_b)
    sgu_b0 = jnp.repeat(sgu_b[0, :, 0], gc_b).reshape(1, d_b)
    y_s, pool_s, v_s = _sample_call(
        x_sample.reshape(dec, d), mod[bsz:], state_pool[0].reshape(dec, POOL_BUF * d_a), norm_gain, w_in_b,
        pool_w_b, pool_b2, pool_scale, sgu_ln_g, sgu_ln_b, sgu_w0, sgu_b0, wba, wbb, wout, fgain)

    return (
        y_p,
        y_s.reshape(dec, 1, d),
        pool_p[None, :, CARRY_ROWS - POOL_BUF:],
        pool_s.reshape(1, dec, POOL_BUF, d_a),
        v_p[None],
        v_s.reshape(1, dec, 1, d_b),
    )
```

```python
import functools

import jax
import jax.numpy as jnp
import numpy as np
from jax import lax
from jax.experimental import pallas as pl
from jax.experimental.pallas import tpu as pltpu

G_A = 4
WINDOWS = (2, 4, 8, 16)
POOL_BUF = max(WINDOWS) - 1
CARRY_ROWS = POOL_BUF + 1
G_B = 4
CHUNK = 128
EPS = 1e-6
SQRT_HALF = np.float32(np.sqrt(0.5))

V7X_LANES = 128
V7X_VMEM_BYTES = 64 * 1024 * 1024

BF16 = jnp.bfloat16
F32 = jnp.float32


def _dot(a, b):
    return lax.dot_general(a, b, (((1,), (0,)), ((), ())), preferred_element_type=F32)


def _gelu(x):
    return 0.5 * x * (1.0 + lax.erf(x * SQRT_HALF))


def _silu(x):
    return x * jax.nn.sigmoid(x)


def _rms_scale(x):
    return lax.rsqrt(jnp.mean(x * x, axis=-1, keepdims=True) + EPS)


def _layernorm(x, g, b):
    mu = jnp.mean(x, axis=-1, keepdims=True)
    xc = x - mu
    var = jnp.mean(xc * xc, axis=-1, keepdims=True)
    return xc * lax.rsqrt(var + EPS) * g + b


def _lane_groups(x, n):
    w = x.shape[-1] // n
    return [x[:, i * w:(i + 1) * w] for i in range(n)]


def _ada_kernel(cp_ref, cs_ref, w_ref, b_ref, op_ref, os_ref):
    w = w_ref[...]
    op_ref[...] = _dot(_silu(cp_ref[...]).astype(BF16), w) + b_ref[...]
    os_ref[...] = _dot(_silu(cs_ref[...]).astype(BF16), w) + b_ref[...]


def _ada_call(c_prompt, c_sample, w_ada, b_ada, *, block_n):
    rp, d = c_prompt.shape
    rs = c_sample.shape[0]
    n = w_ada.shape[1]
    return pl.pallas_call(
        _ada_kernel,
        grid=(n // block_n,),
        in_specs=[
            pl.BlockSpec((rp, d), lambda j: (0, 0)),
            pl.BlockSpec((rs, d), lambda j: (0, 0)),
            pl.BlockSpec((d, block_n), lambda j: (0, j)),
            pl.BlockSpec((1, block_n), lambda j: (0, j)),
        ],
        out_specs=[
            pl.BlockSpec((rp, block_n), lambda j: (0, j)),
            pl.BlockSpec((rs, block_n), lambda j: (0, j)),
        ],
        out_shape=[jax.ShapeDtypeStruct((rp, n), F32), jax.ShapeDtypeStruct((rs, n), F32)],
        compiler_params=pltpu.CompilerParams(dimension_semantics=("arbitrary",)),
        name="ada_mod",
    )(c_prompt, c_sample, w_ada, b_ada)


def _modulated_input(x, mod, gain):
    d = x.shape[-1]
    shift, scale, gate = mod[:, :d], mod[:, d:2 * d], mod[:, 2 * d:]
    h = x * _rms_scale(x) * gain * (1.0 + scale) + shift
    return h, gate


def _pool_project(pooled_minus_a, ga, pool_w_ref, pool_b, pool_scale):
    parts = _lane_groups(pooled_minus_a.astype(BF16), G_A)
    y = jnp.concatenate([_dot(parts[g], pool_w_ref[g]) for g in range(G_A)], axis=-1)
    return (y + pool_b) * pool_scale * _silu(ga)


def _merge_and_output(x, gate, ya, yb, ma, mb, wba_ref, wbb_ref, wout_ref, fgain):
    pa = _dot(ya.astype(BF16), wba_ref[...])
    pb = _dot(yb.astype(BF16), wbb_ref[...])
    merged = jax.nn.sigmoid(ma) * pa + jax.nn.sigmoid(mb) * pb
    xn = x + gate * _dot(merged.astype(BF16), wout_ref[...])
    return xn * _rms_scale(xn) * fgain


def _layer_kernel(
        x_ref, mod_ref,
        xs_ref, mods_ref, state_ref, sgu_w0_ref, sgu_b0_ref,
        gain_ref, w_in_ref, pool_w_ref, pool_b_ref, pool_scale_ref, ln_g_ref, ln_b_ref, sgu_w_ref, sgu_bias_ref,
        wba_ref, wbb_ref, wout_ref, fgain_ref,
        y_ref, pool_out_ref, v_out_ref, ys_ref, state_out_ref, vs_ref,
        band_ref, carry_ref, sgu_wm_ref, prev_ref, *, tm, n_t, n_tiles):
    s = pl.program_id(0)
    d_a = pool_b_ref.shape[-1]
    d_b = ln_g_ref.shape[-1]
    gc = d_a // G_A
    d = x_ref.shape[-1]
    m_lo = 2 * d_a + 3 * d_b

    @pl.when(s == 0)
    def _build_constant_matrices():
        r = lax.broadcasted_iota(jnp.int32, (tm, tm), 0)
        j = lax.broadcasted_iota(jnp.int32, (tm, tm), 1)
        rc = lax.broadcasted_iota(jnp.int32, (CARRY_ROWS, V7X_LANES), 0)
        jc = lax.broadcasted_iota(jnp.int32, (CARRY_ROWS, V7X_LANES), 1)
        for g, w in enumerate(WINDOWS):
            band_ref[g] = jnp.where((r >= j) & (r - j < w), 1.0 / w, 0.0).astype(BF16)
            carry_ref[g] = jnp.where((jc < CARRY_ROWS) & (jc - rc > CARRY_ROWS - w), 1.0 / w, 0.0).astype(BF16)
        rs = lax.broadcasted_iota(jnp.int32, (CHUNK, CHUNK), 0)
        js = lax.broadcasted_iota(jnp.int32, (CHUNK, CHUNK), 1)
        for g in range(G_B):
            sgu_wm_ref[g] = jnp.where(rs >= js, sgu_w_ref[g], 0.0).astype(BF16)

    @pl.when(s < n_tiles)
    def _prompt_tile():
        t = lax.rem(s, n_t)
        x = x_ref[0]
        h, gate = _modulated_input(x, mod_ref[0], gain_ref[...])
        hb = h.astype(BF16)

        def z_cols(lo, width):
            return _dot(hb, w_in_ref[:, lo:lo + width])

        a = z_cols(0, d_a)
        ga = z_cols(d_a, d_a)
        first_tile = t == 0
        prev = jnp.where(first_tile, 0.0, prev_ref[...])
        a_parts = _lane_groups(a.astype(BF16), G_A)
        prev_parts = _lane_groups(prev.astype(BF16), G_A)
        zero_rows = jnp.zeros((V7X_LANES - CARRY_ROWS, gc), BF16)
        pooled_parts = []
        for g in range(G_A):
            main = _dot(band_ref[g], a_parts[g])
            carried = _dot(carry_ref[g], jnp.concatenate([prev_parts[g], zero_rows], axis=0))
            pooled_parts.append(jnp.concatenate([main[:CARRY_ROWS] + carried, main[CARRY_ROWS:]], axis=0))
        pooled = jnp.concatenate(pooled_parts, axis=-1)
        row = lax.broadcasted_iota(jnp.int32, (CARRY_ROWS, d_a), 0)
        lane = lax.broadcasted_iota(jnp.int32, (CARRY_ROWS, d_a), 1)
        win = jnp.full((CARRY_ROWS, d_a), WINDOWS[0], jnp.int32)
        for g, w in enumerate(WINDOWS[1:], start=1):
            win = jnp.where(lane >= g * gc, w, win)
        fix = jnp.where(first_tile, win.astype(F32) / jnp.minimum(row + 1, win).astype(F32), 1.0)
        pooled = jnp.concatenate([pooled[:CARRY_ROWS] * fix, pooled[CARRY_ROWS:]], axis=0)
        ya = _pool_project(pooled - a, ga, pool_w_ref, pool_b_ref[...], pool_scale_ref[...])
        prev_ref[...] = a[tm - CARRY_ROWS:]

        u = _gelu(z_cols(2 * d_a, d_b))
        vn = _layernorm(_gelu(z_cols(2 * d_a + d_b, d_b)), ln_g_ref[...], ln_b_ref[...])
        gb = z_cols(2 * d_a + 2 * d_b, d_b)
        vb = vn.astype(BF16)
        s_chunks = []
        for c in range(tm // CHUNK):
            v_parts = _lane_groups(vb[c * CHUNK:(c + 1) * CHUNK], G_B)
            s_c = jnp.concatenate([_dot(sgu_wm_ref[g], v_parts[g]) for g in range(G_B)], axis=-1)
            s_chunks.append(s_c + sgu_bias_ref[...])
        yb = u * jnp.concatenate(s_chunks, axis=0) * _silu(gb)

        ma = z_cols(m_lo, d)
        mb = z_cols(m_lo + d, d)
        y_ref[0] = _merge_and_output(x, gate, ya, yb, ma, mb, wba_ref, wbb_ref, wout_ref, fgain_ref[...])

        @pl.when(t == n_t - 1)
        def _emit_sequence_state():
            pool_out_ref[0] = a[tm - CARRY_ROWS:]
            v_out_ref[0] = vn[tm - CHUNK:]

    @pl.when(s == n_tiles)
    def _sample_group():
        x = xs_ref[...]
        h, gate = _modulated_input(x, mods_ref[...], gain_ref[...])
        hb = h.astype(BF16)

        def z_cols(lo, width):
            return _dot(hb, w_in_ref[:, lo:lo + width])

        a = z_cols(0, d_a)
        ga = z_cols(d_a, d_a)
        a_parts = _lane_groups(a, G_A)
        pooled_parts = []
        for g, w in enumerate(WINDOWS):
            acc = a_parts[g]
            for j in range(POOL_BUF - (w - 1), POOL_BUF):
                acc = acc + state_ref[j, :, g * gc:(g + 1) * gc]
            pooled_parts.append(acc * (1.0 / w))
        pooled = jnp.concatenate(pooled_parts, axis=-1)
        ya = _pool_project(pooled - a, ga, pool_w_ref, pool_b_ref[...], pool_scale_ref[...])
        for j in range(POOL_BUF - 1):
            state_out_ref[j] = state_ref[j + 1]
        state_out_ref[POOL_BUF - 1] = a

        u = _gelu(z_cols(2 * d_a, d_b))
        vn = _layernorm(_gelu(z_cols(2 * d_a + d_b, d_b)), ln_g_ref[...], ln_b_ref[...])
        gb = z_cols(2 * d_a + 2 * d_b, d_b)
        vs_ref[...] = vn
        yb = u * (vn * sgu_w0_ref[...] + sgu_b0_ref[...]) * _silu(gb)

        ma = z_cols(m_lo, d)
        mb = z_cols(m_lo + d, d)
        ys_ref[...] = _merge_and_output(x, gate, ya, yb, ma, mb, wba_ref, wbb_ref, wout_ref, fgain_ref[...])


def _layer_call(x, mod, xs, mods, state_t, sgu_w0, sgu_b0, gain, w_in, pool_w, pool_b, pool_scale, ln_g, ln_b,
                sgu_w, sgu_bias, wba, wbb, wout, fgain, *, tm):
    bsz, seq, d = x.shape
    dec = xs.shape[0]
    d_in = w_in.shape[1]
    d_a = pool_b.shape[-1]
    d_b = ln_g.shape[-1]
    gc = d_a // G_A
    assert seq % tm == 0 and tm % CHUNK == 0 and seq % CHUNK == 0
    assert gc == V7X_LANES and d_b // G_B == V7X_LANES
    n_t = seq // tm
    n_tiles = bsz * n_t

    def tile(s):
        return jnp.minimum(s, n_tiles - 1)

    def const(shape):
        nd = len(shape)
        return pl.BlockSpec(shape, lambda s: (0,) * nd, pipeline_mode=pl.Buffered(1))

    in_specs = [
        pl.BlockSpec((1, tm, d), lambda s: (tile(s) // n_t, tile(s) % n_t, 0)),
        pl.BlockSpec((1, 1, 3 * d), lambda s: (tile(s) // n_t, 0, 0)),
        const((dec, d)),
        const((dec, 3 * d)),
        const((POOL_BUF, dec, d_a)),
        const((1, d_b)),
        const((1, d_b)),
        const((1, d)),
        const((d, d_in)),
        const((G_A, gc, gc)),
        const((1, d_a)),
        const((1, d_a)),
        const((1, d_b)),
        const((1, d_b)),
        const((G_B, CHUNK, CHUNK)),
        const((CHUNK, d_b)),
        const((d_a, d)),
        const((d_b, d)),
        const((d, d)),
        const((1, d)),
    ]
    out_specs = [
        pl.BlockSpec((1, tm, d), lambda s: (tile(s) // n_t, tile(s) % n_t, 0)),
        pl.BlockSpec((1, CARRY_ROWS, d_a), lambda s: (tile(s) // n_t, 0, 0)),
        pl.BlockSpec((1, CHUNK, d_b), lambda s: (tile(s) // n_t, 0, 0)),
        pl.BlockSpec((dec, d), lambda s: (0, 0)),
        pl.BlockSpec((POOL_BUF, dec, d_a), lambda s: (0, 0, 0)),
        pl.BlockSpec((dec, d_b), lambda s: (0, 0)),
    ]
    out_shape = [
        jax.ShapeDtypeStruct((bsz, seq, d), F32),
        jax.ShapeDtypeStruct((bsz, CARRY_ROWS, d_a), F32),
        jax.ShapeDtypeStruct((bsz, CHUNK, d_b), F32),
        jax.ShapeDtypeStruct((dec, d), F32),
        jax.ShapeDtypeStruct((POOL_BUF, dec, d_a), F32),
        jax.ShapeDtypeStruct((dec, d_b), F32),
    ]
    scratch = [
        pltpu.VMEM((G_A, tm, tm), BF16),
        pltpu.VMEM((G_A, CARRY_ROWS, V7X_LANES), BF16),
        pltpu.VMEM((G_B, CHUNK, CHUNK), BF16),
        pltpu.VMEM((CARRY_ROWS, d_a), F32),
    ]
    return pl.pallas_call(
        functools.partial(_layer_kernel, tm=tm, n_t=n_t, n_tiles=n_tiles),
        grid=(n_tiles + 1,),
        in_specs=in_specs,
        out_specs=out_specs,
        out_shape=out_shape,
        scratch_shapes=scratch,
        compiler_params=pltpu.CompilerParams(
            dimension_semantics=("arbitrary",),
            vmem_limit_bytes=V7X_VMEM_BYTES * 7 // 8,
        ),
        name="layer",
    )(x, mod, xs, mods, state_t, sgu_w0, sgu_b0, gain, w_in, pool_w, pool_b, pool_scale, ln_g, ln_b,
      sgu_w, sgu_bias, wba, wbb, wout, fgain)


def kernel(x_prompt, x_sample, state_pool, c_prompt, c_sample, w_ada, b_ada, norm_gain, w_in, pool_w, pool_b,
           pool_scale, sgu_ln_g, sgu_ln_b, sgu_w, sgu_b, w_branch_a, w_branch_b, w_out, final_gain):
    depth = w_ada.shape[0]
    assert depth == 1, "single-layer trunk only"
    bsz, seq, d = x_prompt.shape
    dec, dec_seq, _ = x_sample.shape
    assert dec_seq == 1
    d_a = pool_b.shape[1] * pool_b.shape[2]
    d_b = sgu_ln_g.shape[1]
    gc_b = d_b // G_B

    mod_p, mod_s = _ada_call(c_prompt, c_sample, w_ada[0], b_ada, block_n=d * 3 // 4)

    sgu_bias = jnp.repeat(sgu_b[0].T, gc_b, axis=1)
    sgu_w0 = jnp.repeat(sgu_w[0, :, 0, 0], gc_b).reshape(1, d_b)
    sgu_b0 = jnp.repeat(sgu_b[0, :, 0], gc_b).reshape(1, d_b)

    y_p, pool_p, v_p, y_s, state_new, v_s = _layer_call(
        x_prompt, mod_p.reshape(bsz, 1, 3 * d), x_sample.reshape(dec, d), mod_s,
        jnp.transpose(state_pool[0], (1, 0, 2)), sgu_w0, sgu_b0, norm_gain, w_in[0], pool_w[0],
        pool_b.reshape(1, d_a), pool_scale, sgu_ln_g, sgu_ln_b, sgu_w[0], sgu_bias,
        w_branch_a[0], w_branch_b[0], w_out[0], final_gain.reshape(1, d), tm=256)

    return (
        y_p,
        y_s.reshape(dec, 1, d),
        pool_p[None, :, CARRY_ROWS - POOL_BUF:],
        jnp.transpose(state_new, (1, 0, 2))[None],
        v_p[None],
        v_s.reshape(1, dec, 1, d_b),
    )
```
